```python
import math, functools
import jax, jax.numpy as jnp
from jax import lax
import numpy as np

D_MODEL = 4096
BATCH = 4
SEQ = 2048
DEPTH = 4
DEC_BATCH = 128
DEC_SEQ = 8
PAST_LEN = 8192
PAGE_SIZE = 128

N_META = 16
EPS = 1e-6
MLA_HEADS = D_MODEL // 128
Q_LORA = D_MODEL // 4
KV_LORA = 256
QK_NOPE = 128
QK_ROPE = 64
V_HEAD = 128
ROPE_THETA = 10000.0
Q_BLOCK = 128
ATTN_SCALE = (QK_NOPE + QK_ROPE) ** -0.5
ML_HEADS = 8
ML_DV = D_MODEL // ML_HEADS
ML_DQK = ML_DV // 2
ML_CHUNK = 64
D_FF = -(-8 * D_MODEL // (3 * 256)) * 256
IN_WIDTHS = (Q_LORA, KV_LORA, QK_ROPE,
             ML_HEADS * ML_DQK, ML_HEADS * ML_DQK, ML_HEADS * ML_DV, ML_HEADS * ML_DV,
             ML_HEADS, ML_HEADS, D_MODEL, D_MODEL)
N_IN = sum(IN_WIDTHS)
IN_SPLITS = tuple(int(s) for s in np.cumsum(IN_WIDTHS)[:-1])

kernel_name = 'hybrid_mla_mlstm_gated_decoder_step'


def rms_norm(x, g):
    xf = x.astype(jnp.float32)
    y = xf * lax.rsqrt(jnp.mean(xf * xf, axis=-1, keepdims=True) + EPS)
    return (y * g.astype(jnp.float32)).astype(x.dtype)


def rope(x, pos):
    half = x.shape[-1] // 2
    inv = ROPE_THETA ** (-jnp.arange(half, dtype=jnp.float32) / half)
    ang = pos.astype(jnp.float32)[:, None] * inv[None, :]
    cos = jnp.cos(ang)[None, :, None, :]
    sin = jnp.sin(ang)[None, :, None, :]
    xf = x.astype(jnp.float32)
    x1, x2 = xf[..., :half], xf[..., half:]
    return jnp.concatenate([x1 * cos - x2 * sin, x2 * cos + x1 * sin], axis=-1).astype(x.dtype)


def causal_mask(n):
    return jnp.tril(jnp.ones((n, n), dtype=bool))


def latent_attention(q_lat, q_rope, key_sets):
    scores = []
    for c, kr, mask in key_sets:
        s = (jnp.einsum('bthr,bsr->bhts', q_lat, c, preferred_element_type=jnp.float32)
             + jnp.einsum('bthp,bsp->bhts', q_rope, kr, preferred_element_type=jnp.float32)) * ATTN_SCALE
        if mask is not None:
            s = jnp.where(mask, s, -jnp.inf)
        scores.append(s)
    s_max = scores[0].max(axis=-1, keepdims=True)
    for s in scores[1:]:
        s_max = jnp.maximum(s_max, s.max(axis=-1, keepdims=True))
    den = None
    acc = None
    for s, (c, _, _) in zip(scores, key_sets):
        p = jnp.exp(s - s_max)
        d = p.sum(axis=-1, keepdims=True)
        o = jnp.einsum('bhts,bsr->bthr', p.astype(c.dtype), c, preferred_element_type=jnp.float32)
        den = d if den is None else den + d
        acc = o if acc is None else acc + o
    return (acc / jnp.swapaxes(den, 1, 2)).astype(q_lat.dtype)


def mla_prompt(q_lat, q_rope, c, kr):
    bsz, total = q_lat.shape[0], q_lat.shape[1]
    n_real = total - N_META
    n_blk = n_real // Q_BLOCK
    meta = latent_attention(q_lat[:, :N_META], q_rope[:, :N_META],
                            [(c[:, :N_META], kr[:, :N_META], causal_mask(N_META))])
    k_pos = jnp.arange(total)

    def to_blocks(a):
        return a[:, N_META:].reshape((bsz, n_blk, Q_BLOCK) + a.shape[2:]).swapaxes(0, 1)

    q_pos = (N_META + jnp.arange(n_real)).reshape(n_blk, Q_BLOCK)

    def one_block(args):
        ql, qr, qp = args
        return latent_attention(ql, qr, [(c, kr, k_pos[None, :] <= qp[:, None])])

    real = lax.map(one_block, (to_blocks(q_lat), to_blocks(q_rope), q_pos))
    real = real.swapaxes(0, 1).reshape((bsz, n_real) + q_lat.shape[2:])
    return jnp.concatenate([meta, real], axis=1)


def mla_sample(q_lat, q_rope, c, kr, past_c, past_kr):
    return latent_attention(q_lat, q_rope,
                            [(past_c, past_kr, None), (c, kr, causal_mask(q_lat.shape[1]))])


def mlstm_chunk(C, n, m, q, k, v, ig, fg):
    f32 = jnp.float32
    C, n, m = C.astype(f32), n.astype(f32), m.astype(f32)
    q = q.astype(f32) * ML_DQK ** -0.5
    k = k.astype(f32)
    v = v.astype(f32)
    L = q.shape[1]
    logf = jnp.swapaxes(jax.nn.log_sigmoid(fg.astype(f32)), 1, 2)
    ig = jnp.swapaxes(ig.astype(f32), 1, 2)
    b = jnp.cumsum(logf, axis=-1)
    dmat = jnp.where(causal_mask(L), b[..., :, None] - b[..., None, :] + ig[..., None, :], -jnp.inf)
    inter = b + m[..., None]
    m_t = jnp.maximum(dmat.max(axis=-1), inter)
    w = jnp.exp(dmat - m_t[..., None]) * jnp.einsum('bthd,bshd->bhts', q, k)
    a = jnp.exp(inter - m_t)
    a_t = jnp.swapaxes(a, 1, 2)[..., None]
    num = jnp.einsum('bhts,bshv->bthv', w, v) + jnp.einsum('bthd,bhdv->bthv', q, C) * a_t
    den = w.sum(axis=-1) + jnp.einsum('bthd,bhd->bht', q, n) * a
    h = num / jnp.swapaxes(jnp.maximum(jnp.abs(den), jnp.exp(-m_t)), 1, 2)[..., None]
    g = b[..., -1:] - b + ig
    m_new = jnp.maximum(b[..., -1] + m, g.max(axis=-1))
    decay = jnp.exp(b[..., -1] + m - m_new)
    kw = k * jnp.swapaxes(jnp.exp(g - m_new[..., None]), 1, 2)[..., None]
    C_new = decay[..., None, None] * C + jnp.einsum('bshd,bshv->bhdv', kw, v)
    n_new = decay[..., None] * n + kw.sum(axis=1)
    return (C_new, n_new, m_new), h


def mlstm_prompt(q, k, v, ig, fg):
    f32 = jnp.float32
    bsz = q.shape[0]
    state = (jnp.zeros((bsz, ML_HEADS, ML_DQK, ML_DV), f32),
             jnp.zeros((bsz, ML_HEADS, ML_DQK), f32),
             jnp.zeros((bsz, ML_HEADS), f32))
    ins = (q, k, v, ig, fg)
    state, h_meta = mlstm_chunk(*state, *(a[:, :N_META] for a in ins))
    n_real = q.shape[1] - N_META
    nc = n_real // ML_CHUNK

    def to_chunks(a):
        return a[:, N_META:].reshape((bsz, nc, ML_CHUNK) + a.shape[2:]).swapaxes(0, 1)

    def step(carry, xs):
        return mlstm_chunk(*carry, *xs)

    state, h_real = lax.scan(step, state, tuple(to_chunks(a) for a in ins))
    h_real = h_real.swapaxes(0, 1).reshape((bsz, n_real) + h_real.shape[3:])
    return state, jnp.concatenate([h_meta, h_real], axis=1)


def block(x, pos, lw, attend, recur):
    (g_pre, w_in_l, g_ql, w_qu, g_kvl, w_ku, w_vu, b_i, b_f, g_mh, w_ba, w_bm, w_o, g_post,
     g_fpre, w_fg, w_fu, w_fd, g_fpost) = lw
    bsz, L, _ = x.shape
    xn = rms_norm(x, g_pre)
    cq, ckv, kr, mq, mk, mv, mo, mi, mf, ga, gb = jnp.split(xn @ w_in_l, IN_SPLITS, axis=-1)
    q = (rms_norm(cq, g_ql) @ w_qu).reshape(bsz, L, MLA_HEADS, QK_NOPE + QK_ROPE)
    q_lat = jnp.einsum('blhd,rhd->blhr', q[..., :QK_NOPE], w_ku)
    q_rope = rope(q[..., QK_NOPE:], pos)
    ckv = rms_norm(ckv, g_kvl)
    kr = rope(kr[:, :, None, :], pos)[:, :, 0, :]
    out_lat = attend(q_lat, q_rope, ckv, kr)
    attn = jnp.einsum('blhr,rhv->blhv', out_lat, w_vu).reshape(bsz, L, MLA_HEADS * V_HEAD)
    state, h = recur(mq.reshape(bsz, L, ML_HEADS, ML_DQK), mk.reshape(bsz, L, ML_HEADS, ML_DQK),
                     mv.reshape(bsz, L, ML_HEADS, ML_DV), mi + b_i, mf + b_f)
    h = h * lax.rsqrt(jnp.mean(h * h, axis=-1, keepdims=True) + EPS) \
        * g_mh.astype(jnp.float32).reshape(ML_HEADS, ML_DV)
    h = h.reshape(bsz, L, ML_HEADS * ML_DV).astype(x.dtype) * jax.nn.sigmoid(mo)
    merged = jax.nn.sigmoid(ga) * (attn @ w_ba) + jax.nn.sigmoid(gb) * (h @ w_bm)
    x = x + rms_norm(merged @ w_o, g_post)
    xf = rms_norm(x, g_fpre)
    x = x + rms_norm((jax.nn.silu(xf @ w_fg) * (xf @ w_fu)) @ w_fd, g_fpost)
    return x, ckv, kr, state


def setup_inputs(seed: int = 0) -> dict:
    key = jax.random.key(seed)
    ks = jax.random.split(key, 32)
    f32 = jnp.float32
    n_pages = PAST_LEN // PAGE_SIZE
    n_used = DEC_BATCH * n_pages
    n_pool = n_used + max(1, n_used // 4)

    def normal(k, shape):
        return jax.random.normal(k, shape, f32)

    def dense(k, shape, fan_in):
        return normal(k, shape) * fan_in ** -0.5

    def gain(k, shape):
        return 1.0 + 0.02 * normal(k, shape)

    page_table = jax.random.permutation(ks[7], n_pool)[:n_used].reshape(DEC_BATCH, n_pages).astype(jnp.int32)
    return {
        'x_prompt': normal(ks[0], (BATCH, SEQ, D_MODEL)),
        'x_sample': normal(ks[1], (DEC_BATCH, DEC_SEQ, D_MODEL)),
        'cache_kv': normal(ks[2], (DEPTH, n_pool, PAGE_SIZE, KV_LORA)),
        'cache_krope': normal(ks[3], (DEPTH, n_pool, PAGE_SIZE, QK_ROPE)),
        'state_C': 0.5 * normal(ks[4], (DEPTH, DEC_BATCH, ML_HEADS, ML_DQK, ML_DV)),
        'state_n': normal(ks[5], (DEPTH, DEC_BATCH, ML_HEADS, ML_DQK)),
        'state_m': normal(ks[6], (DEPTH, DEC_BATCH, ML_HEADS)),
        'page_table': page_table,
        'meta_tokens': normal(ks[8], (N_META, D_MODEL)),
        'g_mix_pre': gain(ks[9], (DEPTH, D_MODEL)),
        'w_in': dense(ks[10], (DEPTH, D_MODEL, N_IN), D_MODEL),
        'g_q_lat': gain(ks[11], (DEPTH, Q_LORA)),
        'w_q_up': dense(ks[12], (DEPTH, Q_LORA, MLA_HEADS * (QK_NOPE + QK_ROPE)), Q_LORA),
        'g_kv_lat': gain(ks[13], (DEPTH, KV_LORA)),
        'w_k_up': dense(ks[14], (DEPTH, KV_LORA, MLA_HEADS, QK_NOPE), KV_LORA),
        'w_v_up': dense(ks[15], (DEPTH, KV_LORA, MLA_HEADS, V_HEAD), KV_LORA),
        'b_igate': 0.1 * normal(ks[16], (DEPTH, ML_HEADS)),
        'b_fgate': 3.0 + 0.5 * normal(ks[17], (DEPTH, ML_HEADS)),
        'g_mlstm_head': gain(ks[18], (DEPTH, ML_HEADS * ML_DV)),
        'w_branch_attn': dense(ks[19], (DEPTH, MLA_HEADS * V_HEAD, D_MODEL), MLA_HEADS * V_HEAD),
        'w_branch_mlstm': dense(ks[20], (DEPTH, ML_HEADS * ML_DV, D_MODEL), ML_HEADS * ML_DV),
        'w_out': dense(ks[21], (DEPTH, D_MODEL, D_MODEL), D_MODEL),
        'g_mix_post': gain(ks[22], (DEPTH, D_MODEL)),
        'g_ffn_pre': gain(ks[23], (DEPTH, D_MODEL)),
        'w_ffn_gate': dense(ks[24], (DEPTH, D_MODEL, D_FF), D_MODEL),
        'w_ffn_up': dense(ks[25], (DEPTH, D_MODEL, D_FF), D_MODEL),
        'w_ffn_down': dense(ks[26], (DEPTH, D_FF, D_MODEL), D_FF),
        'g_ffn_post': gain(ks[27], (DEPTH, D_MODEL)),
    }


def reference(x_prompt, x_sample, cache_kv, cache_krope, state_C, state_n, state_m, page_table,
              meta_tokens, g_mix_pre, w_in, g_q_lat, w_q_up, g_kv_lat, w_k_up, w_v_up, b_igate, b_fgate,
              g_mlstm_head, w_branch_attn, w_branch_mlstm, w_out, g_mix_post, g_ffn_pre, w_ffn_gate,
              w_ffn_up, w_ffn_down, g_ffn_post):
    bsz = x_prompt.shape[0]
    dec_b, dec_t = x_sample.shape[0], x_sample.shape[1]
    meta = jnp.broadcast_to(meta_tokens.astype(x_prompt.dtype)[None], (bsz, N_META, D_MODEL))
    xp = jnp.concatenate([meta, x_prompt], axis=1)
    xs = x_sample
    pos_p = jnp.arange(xp.shape[1])
    pos_s = PAST_LEN + jnp.arange(dec_t)
    kv_p, kr_p, C_p, n_p, m_p = [], [], [], [], []
    kv_s, kr_s, C_s, n_s, m_s = [], [], [], [], []
    for l in range(DEPTH):
        lw = (g_mix_pre[l], w_in[l], g_q_lat[l], w_q_up[l], g_kv_lat[l], w_k_up[l], w_v_up[l],
              b_igate[l], b_fgate[l], g_mlstm_head[l], w_branch_attn[l], w_branch_mlstm[l], w_out[l],
              g_mix_post[l], g_ffn_pre[l], w_ffn_gate[l], w_ffn_up[l], w_ffn_down[l], g_ffn_post[l])
        xp, c_new, kr_new, (C_new, n_new, m_new) = block(xp, pos_p, lw, mla_prompt, mlstm_prompt)
        kv_p.append(c_new)
        kr_p.append(kr_new)
        C_p.append(C_new.astype(x_prompt.dtype))
        n_p.append(n_new.astype(x_prompt.dtype))
        m_p.append(m_new.astype(x_prompt.dtype))
        past_c = cache_kv[l, page_table].reshape(dec_b, -1, KV_LORA)
        past_kr = cache_krope[l, page_table].reshape(dec_b, -1, QK_ROPE)
        attend_s = functools.partial(mla_sample, past_c=past_c, past_kr=past_kr)
        recur_s = functools.partial(mlstm_chunk, state_C[l], state_n[l], state_m[l])
        xs, c_new, kr_new, (C_new, n_new, m_new) = block(xs, pos_s, lw, attend_s, recur_s)
        kv_s.append(c_new)
        kr_s.append(kr_new)
        C_s.append(C_new.astype(state_C.dtype))
        n_s.append(n_new.astype(state_n.dtype))
        m_s.append(m_new.astype(state_m.dtype))
    return (xp[:, N_META:], xs,
            jnp.stack(kv_p), jnp.stack(kr_p), jnp.stack(kv_s), jnp.stack(kr_s),
            jnp.stack(C_p), jnp.stack(n_p), jnp.stack(m_p),
            jnp.stack(C_s), jnp.stack(n_s), jnp.stack(m_s))
```

```python
import functools
import math

import jax
import jax.numpy as jnp
from jax import lax
from jax.experimental import pallas as pl
from jax.experimental.pallas import tpu as pltpu

F32 = jnp.float32
BF16 = jnp.bfloat16

EPS = 1e-6
ROPE_THETA = 10000.0
Q_BLOCK = 128
ML_CHUNK = 64
KEY_BLOCK = 256
SAMPLE_KEY_CHUNK = 1024
LANE = 128
BF16_SUBLANE = 16
VMEM_LIMIT = 56 * 1024 * 1024


def _pick(n, target, mult):
    best = None
    for d in range(mult, min(n, target) + 1, mult):
        if n % d == 0:
            best = d
    if best is None:
        raise ValueError(f"no tile for {n} (target {target}, multiple of {mult})")
    return best


def _params(sem):
    return pltpu.CompilerParams(dimension_semantics=sem, vmem_limit_bytes=VMEM_LIMIT)


def _rms(x, g):
    ms = jnp.mean(x * x, axis=-1, keepdims=True)
    return x * lax.rsqrt(ms + EPS) * g


def _dot(a, b):
    return jnp.dot(a, b, preferred_element_type=F32)


def _dot_nt(a, b):
    return lax.dot_general(a, b, (((1,), (1,)), ((), ())), preferred_element_type=F32)


def _sigmoid(x):
    return 1.0 / (1.0 + jnp.exp(-x))


def _log_sigmoid(x):
    return jnp.minimum(x, 0.0) - jnp.log1p(jnp.exp(-jnp.abs(x)))


def _norm_kernel(x_ref, g_ref, o_ref):
    o_ref[...] = _rms(x_ref[...], g_ref[...]).astype(o_ref.dtype)


def _norm_call(x, g, layer, tm):
    m, d = x.shape
    return pl.pallas_call(
        _norm_kernel,
        grid=(m // tm,),
        in_specs=[pl.BlockSpec((tm, d), lambda i: (i, 0)),
                  pl.BlockSpec((None, 1, d), lambda i: (layer, 0, 0))],
        out_specs=pl.BlockSpec((tm, d), lambda i: (i, 0)),
        out_shape=jax.ShapeDtypeStruct((m, d), BF16),
        compiler_params=_params(("parallel",)),
        name="rmsnorm_rows",
    )(x, g)


def _resnorm_kernel(x_ref, y_ref, g_ref, o_ref):
    o_ref[...] = x_ref[...] + _rms(y_ref[...], g_ref[...])


def _resnorm_call(x, y, g, layer, tm):
    m, d = x.shape
    return pl.pallas_call(
        _resnorm_kernel,
        grid=(m // tm,),
        in_specs=[pl.BlockSpec((tm, d), lambda i: (i, 0)),
                  pl.BlockSpec((tm, d), lambda i: (i, 0)),
                  pl.BlockSpec((None, 1, d), lambda i: (layer, 0, 0))],
        out_specs=pl.BlockSpec((tm, d), lambda i: (i, 0)),
        out_shape=jax.ShapeDtypeStruct((m, d), F32),
        compiler_params=_params(("parallel",)),
        name="residual_rmsnorm",
    )(x, y, g)


def _mm_kernel(*refs, n_a, n_w, n_extra, nk, epilogue):
    a_refs = refs[:n_a]
    w_refs = refs[n_a:n_a + n_w]
    e_refs = refs[n_a + n_w:n_a + n_w + n_extra]
    o_ref = refs[n_a + n_w + n_extra]
    acc_refs = refs[n_a + n_w + n_extra + 1:]
    parts = [_dot(a_refs[k if n_a > 1 else 0][...], w[...]) for k, w in enumerate(w_refs)]
    if nk == 1:
        o_ref[...] = epilogue(parts, [e[...] for e in e_refs]).astype(o_ref.dtype)
        return
    k = pl.program_id(2)

    @pl.when(k == 0)
    def _():
        for acc, p in zip(acc_refs, parts):
            acc[...] = p

    @pl.when(k > 0)
    def _():
        for acc, p in zip(acc_refs, parts):
            acc[...] += p

    @pl.when(k == nk - 1)
    def _():
        o_ref[...] = epilogue([acc[...] for acc in acc_refs],
                              [e[...] for e in e_refs]).astype(o_ref.dtype)


def _mm_call(a_list, w_list, layer, *, n_out, w_col0=0, out_dtype, tm, tn, tk=None,
             epilogue=None, extras=(), extra_col0=(), name):
    m, kdim = a_list[0].shape
    tk = kdim if tk is None else tk
    nk = kdim // tk
    n_w = len(w_list)
    assert len(a_list) in (1, n_w)
    if epilogue is None:
        epilogue = lambda parts, ex: parts[0]
    wj0 = w_col0 // tn
    assert w_col0 % tn == 0 and n_out % tn == 0 and m % tm == 0 and kdim % tk == 0
    in_specs = []
    for _ in a_list:
        in_specs.append(pl.BlockSpec((tm, tk), lambda i, j, k: (i, k)))
    for _ in w_list:
        in_specs.append(pl.BlockSpec((None, tk, tn), lambda i, j, k: (layer, k, wj0 + j)))
    for c0 in extra_col0:
        assert c0 % tn == 0
        in_specs.append(pl.BlockSpec((tm, tn), lambda i, j, k, c0=c0: (i, c0 // tn + j)))
    scratch = [pltpu.VMEM((tm, tn), F32) for _ in range(n_w)] if nk > 1 else []
    return pl.pallas_call(
        functools.partial(_mm_kernel, n_a=len(a_list), n_w=n_w, n_extra=len(extras), nk=nk,
                          epilogue=epilogue),
        grid=(m // tm, n_out // tn, nk),
        in_specs=in_specs,
        out_specs=pl.BlockSpec((tm, tn), lambda i, j, k: (i, j)),
        out_shape=jax.ShapeDtypeStruct((m, n_out), out_dtype),
        scratch_shapes=scratch,
        compiler_params=_params(("parallel", "parallel", "arbitrary")),
        name=name,
    )(*a_list, *w_list, *extras)


def _qprep_kernel(cq_ref, g_ref, wqn_ref, wqr_ref, wqrot_ref, wku_ref, cos_ref, sin_ref,
                  ql_ref, qr_ref, cqn_ref, *, hb, d_nope, d_rope):
    @pl.when(pl.program_id(1) == 0)
    def _():
        cqn_ref[...] = _rms(cq_ref[...], g_ref[...]).astype(BF16)

    cqn = cqn_ref[...]
    qn = _dot(cqn, wqn_ref[...]).astype(BF16)
    qr = _dot(cqn, wqr_ref[...]) * cos_ref[...] + _dot(cqn, wqrot_ref[...]) * sin_ref[...]
    for k in range(hb):
        ql_ref[k] = _dot(qn[:, k * d_nope:(k + 1) * d_nope], wku_ref[k])
        qr_ref[k] = qr[:, k * d_rope:(k + 1) * d_rope]


def _qprep_call(a_pack, g_ql, wqn, wqr, wqrot, wku, cos_q, sin_q, layer, *, tm, hb, q_lora,
                n_heads, d_nope, d_rope, kv_lora):
    m = a_pack.shape[0]
    kern = functools.partial(_qprep_kernel, hb=hb, d_nope=d_nope, d_rope=d_rope)
    return pl.pallas_call(
        kern,
        grid=(m // tm, n_heads // hb),
        in_specs=[
            pl.BlockSpec((tm, q_lora), lambda i, h: (i, 0)),
            pl.BlockSpec((None, 1, q_lora), lambda i, h: (layer, 0, 0)),
            pl.BlockSpec((None, q_lora, hb * d_nope), lambda i, h: (layer, 0, h)),
            pl.BlockSpec((None, q_lora, hb * d_rope), lambda i, h: (layer, 0, h)),
            pl.BlockSpec((None, q_lora, hb * d_rope), lambda i, h: (layer, 0, h)),
            pl.BlockSpec((None, hb, d_nope, kv_lora), lambda i, h: (layer, h, 0, 0)),
            pl.BlockSpec((tm, hb * d_rope), lambda i, h: (i, 0)),
            pl.BlockSpec((tm, hb * d_rope), lambda i, h: (i, 0)),
        ],
        out_specs=[pl.BlockSpec((hb, tm, kv_lora), lambda i, h: (h, i, 0)),
                   pl.BlockSpec((hb, tm, d_rope), lambda i, h: (h, i, 0))],
        out_shape=[jax.ShapeDtypeStruct((n_heads, m, kv_lora), F32),
                   jax.ShapeDtypeStruct((n_heads, m, d_rope), F32)],
        scratch_shapes=[pltpu.VMEM((tm, q_lora), BF16)],
        compiler_params=_params(("parallel", "arbitrary")),
        name="q_prep",
    )(a_pack, g_ql, wqn, wqr, wqrot, wku, cos_q, sin_q)


def _kvprep_kernel(ckv_ref, kr_ref, krrot_ref, g_ref, cos_ref, sin_ref, kv_ref, krope_ref,
                   *, d_rope):
    kv_ref[...] = _rms(ckv_ref[...], g_ref[...])
    kr = kr_ref[...][:, :d_rope]
    krrot = krrot_ref[...][:, :d_rope]
    krope_ref[...] = kr * cos_ref[...] + krrot * sin_ref[...]


def _kvprep_call(a_pack, g_kvl, cos_k, sin_k, layer, *, tm, q_lora, kv_lora, d_rope):
    m = a_pack.shape[0]
    c_kv = q_lora // kv_lora
    c_kr = (q_lora + kv_lora) // LANE
    return pl.pallas_call(
        functools.partial(_kvprep_kernel, d_rope=d_rope),
        grid=(m // tm,),
        in_specs=[
            pl.BlockSpec((tm, kv_lora), lambda i: (i, c_kv)),
            pl.BlockSpec((tm, LANE), lambda i: (i, c_kr)),
            pl.BlockSpec((tm, LANE), lambda i: (i, c_kr + 1)),
            pl.BlockSpec((None, 1, kv_lora), lambda i: (layer, 0, 0)),
            pl.BlockSpec((tm, d_rope), lambda i: (i, 0)),
            pl.BlockSpec((tm, d_rope), lambda i: (i, 0)),
        ],
        out_specs=[pl.BlockSpec((tm, kv_lora), lambda i: (i, 0)),
                   pl.BlockSpec((tm, d_rope), lambda i: (i, 0))],
        out_shape=[jax.ShapeDtypeStruct((m, kv_lora), F32),
                   jax.ShapeDtypeStruct((m, d_rope), F32)],
        compiler_params=_params(("parallel",)),
        name="kv_prep",
    )(a_pack, a_pack, a_pack, g_kvl, cos_k, sin_k)


def _attn_init(q, qr, kc, kr, mask, scale, m_ref, l_ref, acc_ref):
    s = (_dot_nt(q, kc) + _dot_nt(qr, kr)) * scale
    if mask is not None:
        s = jnp.where(mask, s, -jnp.inf)
    m = jnp.max(s, axis=1, keepdims=True)
    p = jnp.exp(s - m)
    m_ref[...] = m
    l_ref[...] = jnp.sum(p, axis=1, keepdims=True)
    acc_ref[...] = _dot(p.astype(BF16), kc)


def _attn_update(q, qr, kc, kr, mask, scale, m_ref, l_ref, acc_ref):
    s = (_dot_nt(q, kc) + _dot_nt(qr, kr)) * scale
    if mask is not None:
        s = jnp.where(mask, s, -jnp.inf)
    m_old = m_ref[...]
    m_new = jnp.maximum(m_old, jnp.max(s, axis=1, keepdims=True))
    alpha = jnp.exp(m_old - m_new)
    p = jnp.exp(s - m_new)
    m_ref[...] = m_new
    l_ref[...] = alpha * l_ref[...] + jnp.sum(p, axis=1, keepdims=True)
    acc_ref[...] = alpha * acc_ref[...] + _dot(p.astype(BF16), kc)


def _attn_prompt_kernel(ql_ref, qr_ref, kc_ref, kr_ref, kcm_ref, krm_ref, o_ref,
                        m_ref, l_ref, acc_ref, *, hg, scale, kv_lora, d_rope):
    qi = pl.program_id(2)
    rows = hg * Q_BLOCK
    q = ql_ref[...].reshape(rows, kv_lora).astype(BF16)
    qr = qr_ref[...].reshape(rows, d_rope).astype(BF16)
    _attn_init(q, qr, kcm_ref[...].astype(BF16), krm_ref[...].astype(BF16), None, scale,
               m_ref, l_ref, acc_ref)
    qpos = qi * Q_BLOCK + (lax.broadcasted_iota(jnp.int32, (rows, KEY_BLOCK), 0) & (Q_BLOCK - 1))
    kofs = lax.broadcasted_iota(jnp.int32, (rows, KEY_BLOCK), 1)
    n_blocks = (qi * Q_BLOCK + Q_BLOCK + KEY_BLOCK - 1) // KEY_BLOCK

    def body(j, carry):
        k0 = pl.multiple_of(j * KEY_BLOCK, KEY_BLOCK)
        kc = kc_ref[pl.ds(k0, KEY_BLOCK), :].astype(BF16)
        kr = kr_ref[pl.ds(k0, KEY_BLOCK), :].astype(BF16)
        mask = (kofs + k0) <= qpos
        _attn_update(q, qr, kc, kr, mask, scale, m_ref, l_ref, acc_ref)
        return carry

    lax.fori_loop(0, n_blocks, body, 0)
    o_ref[...] = (acc_ref[...] / l_ref[...]).reshape(hg, Q_BLOCK, kv_lora)


def _attn_prompt_call(ql, qr, kv, krope, *, hg, n_batch, seq, n_meta, meta_row0, scale):
    n_heads, m, kv_lora = ql.shape
    d_rope = qr.shape[2]
    nq = seq // Q_BLOCK
    rows = hg * Q_BLOCK
    meta_blk0 = meta_row0 // n_meta
    kern = functools.partial(_attn_prompt_kernel, hg=hg, scale=scale, kv_lora=kv_lora,
                             d_rope=d_rope)
    return pl.pallas_call(
        kern,
        grid=(n_batch, n_heads // hg, nq),
        in_specs=[
            pl.BlockSpec((hg, Q_BLOCK, kv_lora), lambda b, g, i: (g, b * nq + i, 0)),
            pl.BlockSpec((hg, Q_BLOCK, d_rope), lambda b, g, i: (g, b * nq + i, 0)),
            pl.BlockSpec((seq, kv_lora), lambda b, g, i: (b, 0)),
            pl.BlockSpec((seq, d_rope), lambda b, g, i: (b, 0)),
            pl.BlockSpec((n_meta, kv_lora), lambda b, g, i: (meta_blk0 + b, 0)),
            pl.BlockSpec((n_meta, d_rope), lambda b, g, i: (meta_blk0 + b, 0)),
        ],
        out_specs=pl.BlockSpec((hg, Q_BLOCK, kv_lora), lambda b, g, i: (g, b * nq + i, 0)),
        out_shape=jax.ShapeDtypeStruct((n_heads, m, kv_lora), F32),
        scratch_shapes=[pltpu.VMEM((rows, 1), F32), pltpu.VMEM((rows, 1), F32),
                        pltpu.VMEM((rows, kv_lora), F32)],
        compiler_params=_params(("parallel", "parallel", "arbitrary")),
        name="attn_prompt",
    )(ql, qr, kv, krope, kv, krope)


def _attn_meta_kernel(prev_ref, ql_ref, qr_ref, kcm_ref, krm_ref, o_ref, *, n_heads, n_meta,
                      scale, kv_lora, d_rope):
    del prev_ref
    rows = n_heads * n_meta
    q = ql_ref[...].reshape(rows, kv_lora).astype(BF16)
    qr = qr_ref[...].reshape(rows, d_rope).astype(BF16)
    kc = kcm_ref[...].astype(BF16)
    kr = krm_ref[...].astype(BF16)
    s = (_dot_nt(q, kc) + _dot_nt(qr, kr)) * scale
    qpos = lax.broadcasted_iota(jnp.int32, (rows, n_meta), 0) & (n_meta - 1)
    kpos = lax.broadcasted_iota(jnp.int32, (rows, n_meta), 1)
    s = jnp.where(kpos <= qpos, s, -jnp.inf)
    m = jnp.max(s, axis=1, keepdims=True)
    p = jnp.exp(s - m)
    den = jnp.sum(p, axis=1, keepdims=True)
    o = _dot(p.astype(BF16), kc) / den
    o_ref[...] = o.reshape(n_heads, n_meta, kv_lora)


def _attn_meta_call(prev, ql, qr, kv, krope, *, n_batch, n_meta, meta_row0, scale):
    n_heads, m, kv_lora = ql.shape
    d_rope = qr.shape[2]
    assert n_meta & (n_meta - 1) == 0
    blk0 = meta_row0 // n_meta
    kern = functools.partial(_attn_meta_kernel, n_heads=n_heads, n_meta=n_meta, scale=scale,
                             kv_lora=kv_lora, d_rope=d_rope)
    return pl.pallas_call(
        kern,
        grid=(n_batch,),
        in_specs=[
            pl.BlockSpec(memory_space=pl.ANY),
            pl.BlockSpec((n_heads, n_meta, kv_lora), lambda b: (0, blk0 + b, 0)),
            pl.BlockSpec((n_heads, n_meta, d_rope), lambda b: (0, blk0 + b, 0)),
            pl.BlockSpec((n_meta, kv_lora), lambda b: (blk0 + b, 0)),
            pl.BlockSpec((n_meta, d_rope), lambda b: (blk0 + b, 0)),
        ],
        out_specs=pl.BlockSpec((n_heads, n_meta, kv_lora), lambda b: (0, blk0 + b, 0)),
        out_shape=jax.ShapeDtypeStruct((n_heads, m, kv_lora), F32),
        input_output_aliases={0: 0},
        compiler_params=_params(("arbitrary",)),
        name="attn_meta",
    )(prev, ql, qr, kv, krope)


def _attn_sample_kernel(pt_ref, prev_ref, ql_ref, qr_ref, kcn_ref, krn_ref, ckv_hbm, ckr_hbm,
                        o_ref, kvbuf, krbuf, sems, m_ref, l_ref, acc_ref, *, layer, n_heads,
                        dec_t, n_pages, page, scale, kv_lora, d_rope, chunk):
    del prev_ref
    b = pl.program_id(0)
    nb = pl.num_programs(0)
    slot = b % 2
    rows = n_heads * dec_t

    def kv_copy(bb, sl, p):
        pid = pt_ref[bb, p]
        r0 = pl.multiple_of(p * page, page)
        return pltpu.make_async_copy(ckv_hbm.at[layer, pid], kvbuf.at[sl, pl.ds(r0, page)],
                                     sems.at[sl, 0])

    def kr_copy(bb, sl, p):
        pid = pt_ref[bb, p]
        r0 = pl.multiple_of(p * page, page)
        return pltpu.make_async_copy(ckr_hbm.at[layer, pid], krbuf.at[sl, pl.ds(r0, page)],
                                     sems.at[sl, 1])

    def start_all(bb, sl):
        def body(p, c):
            kv_copy(bb, sl, p).start()
            kr_copy(bb, sl, p).start()
            return c
        lax.fori_loop(0, n_pages, body, 0)

    def wait_all(bb, sl):
        def body(p, c):
            kv_copy(bb, sl, p).wait()
            kr_copy(bb, sl, p).wait()
            return c
        lax.fori_loop(0, n_pages, body, 0)

    @pl.when(b == 0)
    def _():
        start_all(0, 0)

    @pl.when(b + 1 < nb)
    def _():
        start_all(b + 1, 1 - slot)

    q = ql_ref[...].reshape(rows, kv_lora).astype(BF16)
    qr = qr_ref[...].reshape(rows, d_rope).astype(BF16)
    qpos = lax.broadcasted_iota(jnp.int32, (rows, dec_t), 0) & (dec_t - 1)
    kpos = lax.broadcasted_iota(jnp.int32, (rows, dec_t), 1)
    _attn_init(q, qr, kcn_ref[...].astype(BF16), krn_ref[...].astype(BF16), kpos <= qpos,
               scale, m_ref, l_ref, acc_ref)

    wait_all(b, slot)

    def body(c, carry):
        k0 = pl.multiple_of(c * chunk, chunk)
        kc = kvbuf[slot, pl.ds(k0, chunk), :].astype(BF16)
        kr = krbuf[slot, pl.ds(k0, chunk), :].astype(BF16)
        _attn_update(q, qr, kc, kr, None, scale, m_ref, l_ref, acc_ref)
        return carry

    lax.fori_loop(0, (n_pages * page) // chunk, body, 0)
    o_ref[...] = (acc_ref[...] / l_ref[...]).reshape(n_heads, dec_t, kv_lora)


def _attn_sample_call(prev, page_table, ql, qr, kv, krope, cache_kv, cache_krope, layer, *,
                      dec_b, dec_t, sample_row0, scale):
    n_heads, m, kv_lora = ql.shape
    d_rope = qr.shape[2]
    n_pages = page_table.shape[1]
    page = cache_kv.shape[2]
    past = n_pages * page
    chunk = _pick(past, SAMPLE_KEY_CHUNK, page)
    assert dec_t & (dec_t - 1) == 0
    blk0 = sample_row0 // dec_t
    rows = n_heads * dec_t
    kern = functools.partial(_attn_sample_kernel, layer=layer, n_heads=n_heads, dec_t=dec_t,
                             n_pages=n_pages, page=page, scale=scale, kv_lora=kv_lora,
                             d_rope=d_rope, chunk=chunk)
    grid_spec = pltpu.PrefetchScalarGridSpec(
        num_scalar_prefetch=1,
        grid=(dec_b,),
        in_specs=[
            pl.BlockSpec(memory_space=pl.ANY),
            pl.BlockSpec((n_heads, dec_t, kv_lora), lambda b, pt: (0, blk0 + b, 0)),
            pl.BlockSpec((n_heads, dec_t, d_rope), lambda b, pt: (0, blk0 + b, 0)),
            pl.BlockSpec((dec_t, kv_lora), lambda b, pt: (blk0 + b, 0)),
            pl.BlockSpec((dec_t, d_rope), lambda b, pt: (blk0 + b, 0)),
            pl.BlockSpec(memory_space=pl.ANY),
            pl.BlockSpec(memory_space=pl.ANY),
        ],
        out_specs=pl.BlockSpec((n_heads, dec_t, kv_lora), lambda b, pt: (0, blk0 + b, 0)),
        scratch_shapes=[
            pltpu.VMEM((2, past, kv_lora), F32),
            pltpu.VMEM((2, past, d_rope), F32),
            pltpu.SemaphoreType.DMA((2, 2)),
            pltpu.VMEM((rows, 1), F32), pltpu.VMEM((rows, 1), F32),
            pltpu.VMEM((rows, kv_lora), F32),
        ],
    )
    return pl.pallas_call(
        kern,
        grid_spec=grid_spec,
        out_shape=jax.ShapeDtypeStruct((n_heads, m, kv_lora), F32),
        input_output_aliases={1: 0},
        compiler_params=_params(("arbitrary",)),
        name="attn_sample",
    )(page_table, prev, ql, qr, kv, krope, cache_kv, cache_krope)


def _oprep_kernel(ol_ref, wvu_ref, o_ref, *, hb, v_head):
    for k in range(hb):
        o_ref[:, k * v_head:(k + 1) * v_head] = _dot(ol_ref[k].astype(BF16),
                                                     wvu_ref[k]).astype(o_ref.dtype)


def _oprep_call(ol, wvu, layer, *, tm, hb):
    n_heads, m, kv_lora = ol.shape
    v_head = wvu.shape[-1]
    return pl.pallas_call(
        functools.partial(_oprep_kernel, hb=hb, v_head=v_head),
        grid=(m // tm, n_heads // hb),
        in_specs=[pl.BlockSpec((hb, tm, kv_lora), lambda i, h: (h, i, 0)),
                  pl.BlockSpec((None, hb, kv_lora, v_head), lambda i, h: (layer, h, 0, 0))],
        out_specs=pl.BlockSpec((tm, hb * v_head), lambda i, h: (i, h)),
        out_shape=jax.ShapeDtypeStruct((m, n_heads * v_head), BF16),
        compiler_params=_params(("parallel", "parallel")),
        name="o_prep",
    )(ol, wvu)


def _mlstm_kernel(*refs, sb, chunk, n_chunks, has_init, dqk, dv, n_ml_heads):
    if has_init:
        (q_ref, k_ref, v_ref, gate_ref, bias_ref, og_ref, gmh_ref, c0_ref, n0_ref, m0_ref,
         h_ref, cout_ref, nout_ref, mout_ref, c_sc, n_sc, m_sc) = refs
    else:
        (q_ref, k_ref, v_ref, gate_ref, bias_ref, og_ref, gmh_ref,
         h_ref, cout_ref, nout_ref, mout_ref, c_sc, n_sc, m_sc) = refs
    hd = pl.program_id(1)
    c = pl.program_id(2)
    rows = sb * chunk

    @pl.when(c == 0)
    def _():
        if has_init:
            c_sc[...] = c0_ref[...].reshape(sb, dqk, dv)
            n_sc[...] = n0_ref[...].reshape(sb, 1, dqk)
            m_sc[...] = m0_ref[...].reshape(sb, 1, 1)
        else:
            c_sc[...] = jnp.zeros_like(c_sc)
            n_sc[...] = jnp.zeros_like(n_sc)
            m_sc[...] = jnp.zeros_like(m_sc)

    qf = q_ref[...].astype(F32) * (dqk ** -0.5)
    kf = k_ref[...].astype(F32)
    qb = qf.astype(BF16)
    kb = k_ref[...].astype(BF16)
    vb = v_ref[...].astype(BF16)

    gates = gate_ref[...] + bias_ref[...]
    lane = lax.broadcasted_iota(jnp.int32, (rows, LANE), 1)
    i_col = jnp.sum(jnp.where(lane == hd, gates, 0.0), axis=1, keepdims=True)
    f_col = jnp.sum(jnp.where(lane == n_ml_heads + hd, gates, 0.0), axis=1, keepdims=True)
    logf_col = _log_sigmoid(f_col)

    t_idx = lax.broadcasted_iota(jnp.int32, (rows, rows), 0)
    s_idx = lax.broadcasted_iota(jnp.int32, (rows, rows), 1)
    eye = t_idx == s_idx
    sh = chunk.bit_length() - 1
    if sb == 1:
        same = None
        causal = s_idx <= t_idx
    else:
        same = (t_idx >> sh) == (s_idx >> sh)
        causal = same & (s_idx <= t_idx)

    def to_row(col):
        return jnp.sum(jnp.where(eye, col, 0.0), axis=0, keepdims=True)

    logf_row = to_row(logf_col)
    i_row = to_row(i_col)
    b_col = jnp.sum(jnp.where(causal, logf_row, 0.0), axis=1, keepdims=True)
    b_row = to_row(b_col)
    seg_col = lax.broadcasted_iota(jnp.int32, (rows, 1), 0) >> sh

    def per_row(vals):
        out = vals[0]
        for s in range(1, sb):
            out = jnp.where(seg_col == s, vals[s], out)
        return out + jnp.zeros((rows, 1), F32)

    m_prev = [m_sc[s] for s in range(sb)]
    if same is None:
        blast = [jnp.sum(logf_col, axis=0, keepdims=True)]
    else:
        blast = [jnp.sum(jnp.where(seg_col == s, logf_col, 0.0), axis=0, keepdims=True)
                 for s in range(sb)]
    m_prev_col = per_row(m_prev)
    blast_col = per_row(blast)

    dmat = jnp.where(causal, b_col - b_row + i_row, -jnp.inf)
    inter = b_col + m_prev_col
    m_t = jnp.maximum(jnp.max(dmat, axis=1, keepdims=True), inter)
    w = jnp.exp(dmat - m_t) * _dot_nt(qb, kb)
    a = jnp.exp(inter - m_t)

    qc = None
    qn = None
    for s in range(sb):
        qc_s = _dot(qb, c_sc[s].astype(BF16))
        qn_s = jnp.sum(qf * n_sc[s], axis=1, keepdims=True)
        if s == 0:
            qc, qn = qc_s, qn_s
        else:
            qc = jnp.where(seg_col == s, qc_s, qc)
            qn = jnp.where(seg_col == s, qn_s, qn)
    num = _dot(w.astype(BF16), vb) + qc * a
    den = jnp.sum(w, axis=1, keepdims=True) + qn * a
    h = num / jnp.maximum(jnp.abs(den), jnp.exp(-m_t))

    hn = h * lax.rsqrt(jnp.mean(h * h, axis=-1, keepdims=True) + EPS) * gmh_ref[...]
    h_ref[...] = (hn * og_ref[...]).astype(h_ref.dtype)

    g_col = blast_col - b_col + i_col
    m_new = []
    for s in range(sb):
        g_s = g_col if same is None else jnp.where(seg_col == s, g_col, -jnp.inf)
        m_new.append(jnp.maximum(blast[s] + m_prev[s], jnp.max(g_s, axis=0, keepdims=True)))
    m_new_col = per_row(m_new)
    kw = kf * jnp.exp(g_col - m_new_col)
    for s in range(sb):
        kw_s = kw if same is None else jnp.where(seg_col == s, kw, 0.0)
        decay = jnp.exp(blast[s] + m_prev[s] - m_new[s])
        kwt = kw_s.T.astype(BF16)
        c_sc[s] = decay * c_sc[s] + _dot(kwt, vb)
        n_sc[s] = decay * n_sc[s] + jnp.sum(kw_s, axis=0, keepdims=True)
        m_sc[s] = m_new[s]

    @pl.when(c == n_chunks - 1)
    def _():
        cout_ref[...] = c_sc[...].reshape(cout_ref.shape)
        nout_ref[...] = n_sc[...].reshape(nout_ref.shape)
        mout_ref[...] = m_sc[...].reshape(mout_ref.shape)


def _mlstm_call(h_prev, b_pack, a_pack, gate_bias, c_pack, g_mh, layer, init, state_prev, *,
                n_seq_blocks, sb, chunk, n_chunks, row0, seq_stride, n_ml_heads, dqk, dv,
                q_lora, kv_lora, init_layer=0, state_layer=0, n_state_layers=1):
    m_rows = b_pack.shape[0]
    rows = sb * chunk
    assert row0 % rows == 0 and seq_stride % rows == 0
    blk0 = row0 // rows
    bstride = seq_stride // rows
    gate_cb = (q_lora + kv_lora) // LANE + 2
    qk_blocks = n_ml_heads

    def rowmap(b, h, c):
        return blk0 + b * bstride + c

    in_specs = [
        pl.BlockSpec((rows, dqk), lambda b, h, c: (rowmap(b, h, c), h)),
        pl.BlockSpec((rows, dqk), lambda b, h, c: (rowmap(b, h, c), qk_blocks + h)),
        pl.BlockSpec((rows, dv), lambda b, h, c: (rowmap(b, h, c), (2 * n_ml_heads * dqk) // dv + h)),
        pl.BlockSpec((rows, LANE), lambda b, h, c: (rowmap(b, h, c), gate_cb)),
        pl.BlockSpec((None, 1, LANE), lambda b, h, c: (layer, 0, 0)),
        pl.BlockSpec((rows, dv), lambda b, h, c: (rowmap(b, h, c), h)),
        pl.BlockSpec((None, 1, dv), lambda b, h, c: (layer, 0, h)),
    ]
    args = [b_pack, b_pack, b_pack, a_pack, gate_bias, c_pack, g_mh]
    has_init = init is not None
    il = init_layer
    if has_init:
        in_specs += [
            pl.BlockSpec((None, sb, None, dqk, dv), lambda b, h, c: (il, b, h, 0, 0)),
            pl.BlockSpec((None, sb, None, 1, dqk), lambda b, h, c: (il, b, h, 0, 0)),
            pl.BlockSpec((None, sb, None, 1, 1), lambda b, h, c: (il, b, h, 0, 0)),
        ]
        args += list(init)
    n_seq = n_seq_blocks * sb
    aliases = {}
    n_in = len(args)
    if h_prev is not None:
        in_specs.append(pl.BlockSpec(memory_space=pl.ANY))
        args.append(h_prev)
        aliases[len(args) - 1] = 0
    if state_prev is not None:
        n_layers_out = state_prev[0].shape[0]
        for k, sp in enumerate(state_prev):
            in_specs.append(pl.BlockSpec(memory_space=pl.ANY))
            args.append(sp)
            aliases[len(args) - 1] = 1 + k
    else:
        n_layers_out = n_state_layers
    n_alias = len(args) - n_in
    sl = state_layer
    out_specs = [
        pl.BlockSpec((rows, dv), lambda b, h, c: (rowmap(b, h, c), h)),
        pl.BlockSpec((None, sb, None, dqk, dv), lambda b, h, c: (sl, b, h, 0, 0)),
        pl.BlockSpec((None, sb, None, 1, dqk), lambda b, h, c: (sl, b, h, 0, 0)),
        pl.BlockSpec((None, sb, None, 1, 1), lambda b, h, c: (sl, b, h, 0, 0)),
    ]
    out_shape = [
        jax.ShapeDtypeStruct((m_rows, n_ml_heads * dv), BF16),
        jax.ShapeDtypeStruct((n_layers_out, n_seq, n_ml_heads, dqk, dv), F32),
        jax.ShapeDtypeStruct((n_layers_out, n_seq, n_ml_heads, 1, dqk), F32),
        jax.ShapeDtypeStruct((n_layers_out, n_seq, n_ml_heads, 1, 1), F32),
    ]
    kern = functools.partial(_mlstm_kernel, sb=sb, chunk=chunk, n_chunks=n_chunks,
                             has_init=has_init, dqk=dqk, dv=dv, n_ml_heads=n_ml_heads)

    def wrapped(*refs):
        kern(*refs[:n_in], *refs[n_in + n_alias:])

    return pl.pallas_call(
        wrapped,
        grid=(n_seq_blocks, n_ml_heads, n_chunks),
        in_specs=in_specs,
        out_specs=out_specs,
        out_shape=out_shape,
        scratch_shapes=[pltpu.VMEM((sb, dqk, dv), F32), pltpu.VMEM((sb, 1, dqk), F32),
                        pltpu.VMEM((sb, 1, 1), F32)],
        input_output_aliases=aliases,
        compiler_params=_params(("parallel", "parallel", "arbitrary")),
        name=f"mlstm_c{chunk}_s{sb}",
    )(*args)


def _pad_cols(w, width):
    return jnp.pad(w, ((0, 0), (0, 0), (0, width - w.shape[-1])))


def _rot_half_cols(w, d_rope):
    shp = w.shape
    w = w.reshape(shp[:-1] + (shp[-1] // d_rope, 2, d_rope // 2))
    w = jnp.stack([-w[..., 1, :], w[..., 0, :]], axis=-2)
    return w.reshape(shp)


def kernel(x_prompt, x_sample, cache_kv, cache_krope, state_C, state_n, state_m, page_table, meta_tokens, g_mix_pre, w_in, g_q_lat, w_q_up, g_kv_lat, w_k_up, w_v_up, b_igate, b_fgate, g_mlstm_head, w_branch_attn, w_branch_mlstm, w_out, g_mix_post, g_ffn_pre, w_ffn_gate, w_ffn_up, w_ffn_down, g_ffn_post):
    n_batch, seq, d_model = x_prompt.shape
    dec_b, dec_t, _ = x_sample.shape
    depth = w_in.shape[0]
    n_meta = meta_tokens.shape[0]
    q_lora = g_q_lat.shape[1]
    kv_lora, n_heads, d_nope = w_k_up.shape[1:]
    v_head = w_v_up.shape[3]
    d_rope = cache_krope.shape[-1]
    n_ml_heads = b_igate.shape[1]
    dqk = state_n.shape[-1]
    dv = state_C.shape[-1]
    d_ff = w_ffn_gate.shape[-1]
    page = cache_kv.shape[2]
    past_len = page_table.shape[1] * page
    scale = (d_nope + d_rope) ** -0.5
    half = d_rope // 2

    n_real = n_batch * seq
    sample_row0 = n_real
    meta_row0 = n_real + dec_b * dec_t
    m_rows = meta_row0 + n_batch * n_meta
    assert q_lora % kv_lora == 0 and (q_lora + kv_lora) % LANE == 0 and d_rope <= LANE
    assert 2 * n_ml_heads <= LANE and seq % Q_BLOCK == 0 and seq % ML_CHUNK == 0

    x = jnp.concatenate([
        x_prompt.reshape(n_real, d_model),
        x_sample.reshape(dec_b * dec_t, d_model),
        jnp.broadcast_to(meta_tokens.astype(x_prompt.dtype)[None], (n_batch, n_meta, d_model)
                         ).reshape(n_batch * n_meta, d_model)], axis=0)

    pos = jnp.concatenate([
        jnp.tile(n_meta + jnp.arange(seq), n_batch),
        jnp.tile(past_len + jnp.arange(dec_t), dec_b),
        jnp.tile(jnp.arange(n_meta), n_batch)]).astype(F32)
    inv = ROPE_THETA ** (-jnp.arange(half, dtype=F32) / half)
    ang = pos[:, None] * inv[None, :]
    cos_k = jnp.concatenate([jnp.cos(ang), jnp.cos(ang)], axis=1)
    sin_k = jnp.concatenate([jnp.sin(ang), jnp.sin(ang)], axis=1)
    hb = _pick(n_heads, 8, 1)
    cos_q = jnp.tile(cos_k, (1, hb))
    sin_q = jnp.tile(sin_k, (1, hb))

    splits = [q_lora, kv_lora, d_rope, n_ml_heads * dqk, n_ml_heads * dqk, n_ml_heads * dv,
              n_ml_heads * dv, n_ml_heads, n_ml_heads, d_model, d_model]
    offs = [0]
    for s in splits:
        offs.append(offs[-1] + s)
    seg = lambda k: w_in[:, :, offs[k]:offs[k + 1]]
    w_kr = seg(2)
    w_a = jnp.concatenate([
        seg(0), seg(1), _pad_cols(w_kr, LANE), _pad_cols(_rot_half_cols(w_kr, d_rope), LANE),
        _pad_cols(jnp.concatenate([seg(7), seg(8)], axis=-1), 2 * LANE)], axis=-1).astype(BF16)
    w_b = jnp.concatenate([seg(3), seg(4), seg(5)], axis=-1).astype(BF16)
    w_c = jnp.concatenate([seg(6), seg(9), seg(10)], axis=-1).astype(BF16)
    n_a = w_a.shape[-1]
    gate_bias = _pad_cols(jnp.concatenate([b_igate, b_fgate], axis=-1)[:, None, :], LANE)

    wq = w_q_up.reshape(depth, q_lora, n_heads, d_nope + d_rope)
    wqn = wq[..., :d_nope].reshape(depth, q_lora, n_heads * d_nope).astype(BF16)
    wqr_f = wq[..., d_nope:].reshape(depth, q_lora, n_heads * d_rope)
    wqr = wqr_f.astype(BF16)
    wqrot = _rot_half_cols(wqr_f, d_rope).astype(BF16)
    wku = jnp.transpose(w_k_up, (0, 2, 3, 1)).astype(BF16)
    wvu = jnp.transpose(w_v_up, (0, 2, 1, 3)).astype(BF16)
    w_ba = w_branch_attn.astype(BF16)
    w_bm = w_branch_mlstm.astype(BF16)
    w_o = w_out.astype(BF16)
    ff_pad = -(-d_ff // 1024) * 1024 if d_ff > 1024 else -(-d_ff // LANE) * LANE
    w_fg = _pad_cols(w_ffn_gate, ff_pad).astype(BF16)
    w_fu = _pad_cols(w_ffn_up, ff_pad).astype(BF16)
    w_fd = jnp.pad(w_ffn_down, ((0, 0), (0, ff_pad - d_ff), (0, 0))).astype(BF16)

    g3 = lambda g: g[:, None, :]
    g_pre, g_ql, g_kvl, g_mh = g3(g_mix_pre), g3(g_q_lat), g3(g_kv_lat), g3(g_mlstm_head)
    g_post, g_fpre, g_fpost = g3(g_mix_post), g3(g_ffn_pre), g3(g_ffn_post)

    state_n5 = state_n[:, :, :, None, :]
    state_m5 = state_m[:, :, :, None, None]

    tm = _pick(m_rows, 1024, BF16_SUBLANE)
    tm_row = _pick(m_rows, 512, BF16_SUBLANE)
    tn_big = lambda n: _pick(n, 1024, LANE)
    hg = _pick(n_heads, 8, 1)
    sb_sample = max(1, BF16_SUBLANE // dec_t)
    assert dec_b % sb_sample == 0 and n_meta % BF16_SUBLANE == 0

    kv_p, kr_p, c_p, n_p, m_p = [], [], [], [], []
    kv_s, kr_s = [], []
    cs_state = None
    sigm = lambda parts, ex: _sigmoid(parts[0])
    swiglu = lambda parts, ex: (parts[0] * _sigmoid(parts[0])) * parts[1]
    merge = lambda parts, ex: ex[0] * parts[0] + ex[1] * parts[1]

    for l in range(depth):
        xn = _norm_call(x, g_pre, l, tm_row)
        a_pack = _mm_call([xn], [w_a], l, n_out=n_a, out_dtype=F32, tm=tm, tn=_pick(n_a, 1024, LANE),
                          name="in_proj_a")
        b_pack = _mm_call([xn], [w_b], l, n_out=w_b.shape[-1], out_dtype=BF16, tm=tm,
                          tn=tn_big(w_b.shape[-1]), name="in_proj_b")
        c_pack = _mm_call([xn], [w_c], l, n_out=w_c.shape[-1], out_dtype=F32, tm=tm,
                          tn=tn_big(w_c.shape[-1]), epilogue=sigm, name="in_proj_c")

        ql, qr = _qprep_call(a_pack, g_ql, wqn, wqr, wqrot, wku, cos_q, sin_q, l, tm=tm, hb=hb,
                             q_lora=q_lora, n_heads=n_heads, d_nope=d_nope, d_rope=d_rope,
                             kv_lora=kv_lora)
        kv, krope = _kvprep_call(a_pack, g_kvl, cos_k, sin_k, l, tm=tm_row, q_lora=q_lora,
                                 kv_lora=kv_lora, d_rope=d_rope)
        ol = _attn_prompt_call(ql, qr, kv, krope, hg=hg, n_batch=n_batch, seq=seq,
                               n_meta=n_meta, meta_row0=meta_row0, scale=scale)
        ol = _attn_meta_call(ol, ql, qr, kv, krope, n_batch=n_batch, n_meta=n_meta,
                             meta_row0=meta_row0, scale=scale)
        ol = _attn_sample_call(ol, page_table, ql, qr, kv, krope, cache_kv, cache_krope, l,
                               dec_b=dec_b, dec_t=dec_t, sample_row0=sample_row0, scale=scale)
        attn = _oprep_call(ol, wvu, l, tm=tm, hb=hb)

        common = dict(n_ml_heads=n_ml_heads, dqk=dqk, dv=dv, q_lora=q_lora, kv_lora=kv_lora)
        ml_in = (b_pack, a_pack, gate_bias, c_pack, g_mh, l)
        hm, c1, n1, m1 = _mlstm_call(None, *ml_in, None, None, n_seq_blocks=n_batch, sb=1,
                                     chunk=n_meta, n_chunks=1, row0=meta_row0,
                                     seq_stride=n_meta, **common)
        hm, c2, n2, m2 = _mlstm_call(hm, *ml_in, (c1, n1, m1), None, n_seq_blocks=n_batch, sb=1,
                                     chunk=ML_CHUNK, n_chunks=seq // ML_CHUNK, row0=0,
                                     seq_stride=seq, **common)
        c_p.append(c2[0])
        n_p.append(n2[0, :, :, 0, :])
        m_p.append(m2[0, :, :, 0, 0])
        hm, cs, ns, ms = _mlstm_call(hm, *ml_in, (state_C, state_n5, state_m5), cs_state,
                                     n_seq_blocks=dec_b // sb_sample, sb=sb_sample, chunk=dec_t,
                                     n_chunks=1, row0=sample_row0, seq_stride=sb_sample * dec_t,
                                     init_layer=l, state_layer=l, n_state_layers=depth, **common)
        cs_state = (cs, ns, ms)

        tn_d = tn_big(d_model)
        merged = _mm_call([attn, hm], [w_ba, w_bm], l, n_out=d_model, out_dtype=BF16, tm=tm_row,
                          tn=_pick(d_model, 512, LANE), epilogue=merge, extras=(c_pack, c_pack),
                          extra_col0=(n_ml_heads * dv, n_ml_heads * dv + d_model), name="merge")
        y = _mm_call([merged], [w_o], l, n_out=d_model, out_dtype=F32, tm=tm, tn=tn_d,
                     name="out_proj")
        x = _resnorm_call(x, y, g_post, l, tm_row)

        xf = _norm_call(x, g_fpre, l, tm_row)
        hh = _mm_call([xf], [w_fg, w_fu], l, n_out=ff_pad, out_dtype=BF16, tm=tm,
                      tn=_pick(ff_pad, 512, LANE), epilogue=swiglu, name="ffn_gate_up")
        y = _mm_call([hh], [w_fd], l, n_out=d_model, out_dtype=F32, tm=tm, tn=tn_d,
                     tk=_pick(ff_pad, 4096, LANE), name="ffn_down")
        x = _resnorm_call(x, y, g_fpost, l, tm_row)

        kv_real = kv[:n_real].reshape(n_batch, seq, kv_lora)
        kv_meta = kv[meta_row0:].reshape(n_batch, n_meta, kv_lora)
        kr_real = krope[:n_real].reshape(n_batch, seq, d_rope)
        kr_meta = krope[meta_row0:].reshape(n_batch, n_meta, d_rope)
        kv_p.append(jnp.concatenate([kv_meta, kv_real], axis=1))
        kr_p.append(jnp.concatenate([kr_meta, kr_real], axis=1))
        kv_s.append(kv[sample_row0:meta_row0].reshape(dec_b, dec_t, kv_lora))
        kr_s.append(krope[sample_row0:meta_row0].reshape(dec_b, dec_t, d_rope))

    cs, ns, ms = cs_state
    return (x[:n_real].reshape(n_batch, seq, d_model),
            x[sample_row0:meta_row0].reshape(dec_b, dec_t, d_model),
            jnp.stack(kv_p), jnp.stack(kr_p), jnp.stack(kv_s), jnp.stack(kr_s),
            jnp.stack(c_p), jnp.stack(n_p), jnp.stack(m_p),
            cs, ns[:, :, :, 0, :], ms[:, :, :, 0, 0])
```
